```python
import jax, jax.numpy as jnp
from jax import lax
import numpy as np

D_MODEL = 1024
BATCH = 8
SEQ = 4096
DEPTH = 4

POOL_WIDTH = D_MODEL // 2
POOL_WINDOWS = (2, 4, 8, 16)
N_POOL_GROUPS = 4
POOL_GROUP = POOL_WIDTH // N_POOL_GROUPS
POOL_MAX_WINDOW = 16
CONV_WIDTH = D_MODEL // 2
CONV_KERNEL = 31
LRU_WIDTH = D_MODEL
LRU_HEADS = 4
LRU_BLOCK = LRU_WIDTH // LRU_HEADS
LRU_CONV = 4
LRU_C = 8.0
N_BRANCHES = 3
IN_WIDTH = 2 * POOL_WIDTH + 3 * CONV_WIDTH + 2 * LRU_WIDTH + N_BRANCHES * D_MODEL
EPS = 1e-6

kernel_name = "hybrid_pool_conv_rglru_gated_trunk"


def rmsnorm(x, g):
    xf = x.astype(jnp.float32)
    y = xf * lax.rsqrt(jnp.mean(xf * xf, axis=-1, keepdims=True) + EPS)
    return (y * g.astype(jnp.float32)).astype(x.dtype)


def layernorm(x, g, b):
    xf = x.astype(jnp.float32)
    mu = jnp.mean(xf, axis=-1, keepdims=True)
    var = jnp.mean(jnp.square(xf - mu), axis=-1, keepdims=True)
    y = (xf - mu) * lax.rsqrt(var + EPS)
    return (y * g.astype(jnp.float32) + b.astype(jnp.float32)).astype(x.dtype)


def causal_depthwise_conv(x, w, b):
    k = w.shape[0]
    c = x.shape[-1]
    y = lax.conv_general_dilated(
        x, w[:, None, :].astype(x.dtype), window_strides=(1,), padding=[(k - 1, 0)],
        dimension_numbers=("NWC", "WIO", "NWC"), feature_group_count=c)
    return y + b.astype(x.dtype)


def pool_mixer(u, w_grp, scale):
    bsz, s, _ = u.shape
    uf = u.astype(jnp.float32).reshape(bsz, s, N_POOL_GROUPS, POOL_GROUP)
    csum = jnp.cumsum(uf, axis=1)
    cpad = jnp.pad(csum, ((0, 0), (POOL_MAX_WINDOW, 0), (0, 0), (0, 0)))
    t = jnp.arange(s)
    means = []
    for g, w in enumerate(POOL_WINDOWS):
        lag = lax.slice_in_dim(cpad, POOL_MAX_WINDOW - w, POOL_MAX_WINDOW - w + s, axis=1)[:, :, g]
        cnt = jnp.minimum(t + 1, w).astype(jnp.float32)[None, :, None]
        means.append((csum[:, :, g] - lag) / cnt)
    pooled = (jnp.stack(means, axis=2) - uf).astype(u.dtype)
    y = jnp.einsum("bsgc,gcd->bsgd", pooled, w_grp)
    return y.reshape(bsz, s, POOL_WIDTH) * scale


def rg_lru(x, w_a, b_a, w_x, b_x, lam):
    bsz, s, _ = x.shape
    xh = x.reshape(bsz, s, LRU_HEADS, LRU_BLOCK)
    r = jax.nn.sigmoid(jnp.einsum("bshi,hij->bshj", xh, w_a).reshape(bsz, s, LRU_WIDTH) + b_a)
    i = jax.nn.sigmoid(jnp.einsum("bshi,hij->bshj", xh, w_x).reshape(bsz, s, LRU_WIDTH) + b_x)
    log_a = -LRU_C * r.astype(jnp.float32) * jax.nn.softplus(-lam.astype(jnp.float32))
    a = jnp.exp(log_a)
    mult = jnp.sqrt(-jnp.expm1(2.0 * log_a))
    bterm = mult * (i * x).astype(jnp.float32)

    def combine(left, right):
        a1, b1 = left
        a2, b2 = right
        return a1 * a2, a2 * b1 + b2

    _, h = lax.associative_scan(combine, (a, bterm), axis=1)
    return h.astype(x.dtype)


def hybrid_layer(x, norm_pre, w_in, pool_w, pool_scale, w_pool_out, conv_dw, conv_b,
                 conv_ln_g, conv_ln_b, w_conv_out, lru_conv_w, lru_conv_b, lru_wa, lru_ba,
                 lru_wx, lru_bx, lru_lambda, w_lru_out, w_out, norm_post):
    h = rmsnorm(x, norm_pre)
    z = h @ w_in
    o = 0
    def take(n):
        nonlocal o
        piece = z[..., o:o + n]
        o += n
        return piece
    p_val, p_gate = take(POOL_WIDTH), take(POOL_WIDTH)
    c_val, c_glu, c_gate = take(CONV_WIDTH), take(CONV_WIDTH), take(CONV_WIDTH)
    r_val, r_gate = take(LRU_WIDTH), take(LRU_WIDTH)
    g_pool, g_conv, g_lru = take(D_MODEL), take(D_MODEL), take(D_MODEL)

    y_pool = (pool_mixer(p_val, pool_w, pool_scale) * jax.nn.silu(p_gate)) @ w_pool_out

    c = c_val * jax.nn.sigmoid(c_glu)
    c = causal_depthwise_conv(c, conv_dw, conv_b)
    c = jax.nn.silu(layernorm(c, conv_ln_g, conv_ln_b))
    y_conv = (c * jax.nn.silu(c_gate)) @ w_conv_out

    r = causal_depthwise_conv(r_val, lru_conv_w, lru_conv_b)
    r = rg_lru(r, lru_wa, lru_ba, lru_wx, lru_bx, lru_lambda)
    y_lru = (r * jax.nn.silu(r_gate)) @ w_lru_out

    merged = (jax.nn.sigmoid(g_pool) * y_pool + jax.nn.sigmoid(g_conv) * y_conv
              + jax.nn.sigmoid(g_lru) * y_lru)
    out = merged @ w_out
    return x + rmsnorm(out, norm_post)


def setup_inputs(seed: int = 0) -> dict:
    key = jax.random.key(seed)
    ks = jax.random.split(key, 24)
    f32 = jnp.float32
    def nrm(k, shape, fan_in):
        return jax.random.normal(k, shape, f32) * (fan_in ** -0.5)
    def small(k, shape, s=0.01):
        return jax.random.normal(k, shape, f32) * s
    a0 = jax.random.uniform(ks[17], (DEPTH, LRU_WIDTH), f32, 0.9, 0.999)
    sig = a0 ** (1.0 / LRU_C)
    lru_lambda = jnp.log(sig) - jnp.log1p(-sig)
    return {
        "x": jax.random.normal(ks[0], (BATCH, SEQ, D_MODEL), f32),
        "norm_pre": 1.0 + small(ks[1], (DEPTH, D_MODEL), 0.05),
        "w_in": nrm(ks[2], (DEPTH, D_MODEL, IN_WIDTH), D_MODEL),
        "pool_w": nrm(ks[3], (DEPTH, N_POOL_GROUPS, POOL_GROUP, POOL_GROUP), POOL_GROUP),
        "pool_scale": 1.0 + small(ks[4], (DEPTH, POOL_WIDTH), 0.05),
        "w_pool_out": nrm(ks[5], (DEPTH, POOL_WIDTH, D_MODEL), POOL_WIDTH),
        "conv_dw": nrm(ks[6], (DEPTH, CONV_KERNEL, CONV_WIDTH), CONV_KERNEL),
        "conv_b": small(ks[7], (DEPTH, CONV_WIDTH)),
        "conv_ln_g": 1.0 + small(ks[8], (DEPTH, CONV_WIDTH), 0.05),
        "conv_ln_b": small(ks[9], (DEPTH, CONV_WIDTH)),
        "w_conv_out": nrm(ks[10], (DEPTH, CONV_WIDTH, D_MODEL), CONV_WIDTH),
        "lru_conv_w": nrm(ks[11], (DEPTH, LRU_CONV, LRU_WIDTH), LRU_CONV),
        "lru_conv_b": small(ks[12], (DEPTH, LRU_WIDTH)),
        "lru_wa": nrm(ks[13], (DEPTH, LRU_HEADS, LRU_BLOCK, LRU_BLOCK), LRU_BLOCK),
        "lru_ba": small(ks[14], (DEPTH, LRU_WIDTH)),
        "lru_wx": nrm(ks[15], (DEPTH, LRU_HEADS, LRU_BLOCK, LRU_BLOCK), LRU_BLOCK),
        "lru_bx": small(ks[16], (DEPTH, LRU_WIDTH)),
        "lru_lambda": lru_lambda,
        "w_lru_out": nrm(ks[18], (DEPTH, LRU_WIDTH, D_MODEL), LRU_WIDTH),
        "w_out": nrm(ks[19], (DEPTH, D_MODEL, D_MODEL), D_MODEL),
        "norm_post": 1.0 + small(ks[20], (DEPTH, D_MODEL), 0.05),
    }


def reference(x, norm_pre, w_in, pool_w, pool_scale, w_pool_out, conv_dw, conv_b, conv_ln_g,
              conv_ln_b, w_conv_out, lru_conv_w, lru_conv_b, lru_wa, lru_ba, lru_wx, lru_bx,
              lru_lambda, w_lru_out, w_out, norm_post):
    for l in range(DEPTH):
        x = hybrid_layer(x, norm_pre[l], w_in[l], pool_w[l], pool_scale[l], w_pool_out[l],
                         conv_dw[l], conv_b[l], conv_ln_g[l], conv_ln_b[l], w_conv_out[l],
                         lru_conv_w[l], lru_conv_b[l], lru_wa[l], lru_ba[l], lru_wx[l],
                         lru_bx[l], lru_lambda[l], w_lru_out[l], w_out[l], norm_post[l])
    return x
```

```python
import functools

import jax
import jax.numpy as jnp
from jax import lax
from jax.experimental import pallas as pl
from jax.experimental.pallas import tpu as pltpu

D_MODEL = 1024
BATCH = 8
POOL_WIDTH = 512
POOL_WINDOWS = (2, 4, 8, 16)
POOL_GROUP = 128
CONV_WIDTH = 512
CONV_KERNEL = 31
LRU_WIDTH = 1024
LRU_HEADS = 4
LRU_BLOCK = 256
LRU_CONV = 4
LRU_C = 8.0
EPS = 1e-6

O_PVAL, O_PGATE = 0, 512
O_CVAL, O_CGLU, O_CGATE = 1024, 1536, 2048
O_RVAL, O_RGATE = 2560, 3584
O_GPOOL, O_GCONV, O_GLRU = 4608, 5632, 6656
IN_WIDTH = 7680

SUBLANES = 8
TILE_T = 64
TILE_M = TILE_T * BATCH
HIST_P = (max(POOL_WINDOWS) - 1) * BATCH
HIST_C = (CONV_KERNEL - 1) * BATCH
HIST_R = (LRU_CONV - 1) * BATCH
VMEM_LIMIT_BYTES = 60 * 1024 * 1024

F32 = jnp.float32
BF16 = jnp.bfloat16


def _sigmoid(v):
    return 1.0 / (1.0 + jnp.exp(-v))


def _silu(v):
    return v * _sigmoid(v)


def _rows(i, n):
    return pl.ds(pl.multiple_of(i * n, n), n)


def _layer_kernel(x_ref, npre_ref, win_ref, pw_ref, pscale_ref, wpo_ref, cdw_ref, cb_ref,
                  lng_ref, lnb_ref, wco_ref, rcw_ref, rcb_ref, wa_ref, ba_ref, wx_ref, bx_ref,
                  lam_ref, wlo_ref, wout_ref, npost_ref,
                  o_ref,
                  h_ref, z_ref, y_ref, g_ref, m_ref, act_ref, rc_ref, pbuf, cbuf, rbuf, hs_ref):
    step = pl.program_id(0)
    tm = TILE_M

    @pl.when(step == 0)
    def _():
        pbuf[0:HIST_P, :] = jnp.zeros((HIST_P, POOL_WIDTH), F32)
        cbuf[0:HIST_C, :] = jnp.zeros((HIST_C, CONV_WIDTH), F32)
        rbuf[0:HIST_R, :] = jnp.zeros((HIST_R, LRU_WIDTH), F32)
        hs_ref[...] = jnp.zeros((SUBLANES, LRU_WIDTH), F32)

    def dot(a, b):
        return jnp.dot(a, b, preferred_element_type=F32)

    rn = 32

    def norm_body(i, c):
        r = _rows(i, rn)
        x = x_ref[r, :]
        ms = jnp.mean(x * x, axis=-1, keepdims=True)
        h_ref[r, :] = (x * lax.rsqrt(ms + EPS) * npre_ref[...]).astype(BF16)
        return c

    lax.fori_loop(0, tm // rn, norm_body, 0)

    def merge(first):
        rm = 32

        def body(i, c):
            r = _rows(i, rm)
            v = _sigmoid(g_ref[r, :]) * y_ref[r, :]
            if first:
                m_ref[r, :] = v
            else:
                m_ref[r, :] = m_ref[r, :] + v
            return c

        lax.fori_loop(0, tm // rm, body, 0)

    z_ref[:, 0:1024] = dot(h_ref[...], win_ref[:, O_PVAL:O_PVAL + 1024])
    pbuf[HIST_P:HIST_P + tm, :] = z_ref[:, 0:POOL_WIDTH]
    rp = 32

    def pool_body(i, c):
        row0 = pl.multiple_of(i * rp, rp)
        tv = (step * TILE_T + 1
              + (row0 + lax.broadcasted_iota(jnp.int32, (rp, POOL_GROUP), 0)) // BATCH).astype(F32)
        outs = []
        for grp, w in enumerate(POOL_WINDOWS):
            lanes = slice(grp * POOL_GROUP, (grp + 1) * POOL_GROUP)
            u = pbuf[pl.ds(pl.multiple_of(row0 + HIST_P, SUBLANES), rp), lanes]
            acc = u
            for j in range(1, w):
                acc = acc + pbuf[pl.ds(pl.multiple_of(row0 + (HIST_P - BATCH * j), SUBLANES), rp), lanes]
            outs.append(acc / jnp.minimum(tv, float(w)) - u)
        act_ref[pl.ds(row0, rp), 0:POOL_WIDTH] = jnp.concatenate(outs, axis=-1).astype(BF16)
        return c

    lax.fori_loop(0, tm // rp, pool_body, 0)
    y_ref[:, 0:POOL_WIDTH] = dot(act_ref[:, 0:POOL_WIDTH], pw_ref[...])

    def pool_act_body(i, c):
        r = _rows(i, rp)
        v = y_ref[r, 0:POOL_WIDTH] * pscale_ref[...] * _silu(z_ref[r, POOL_WIDTH:2 * POOL_WIDTH])
        act_ref[r, 0:POOL_WIDTH] = v.astype(BF16)
        return c

    lax.fori_loop(0, tm // rp, pool_act_body, 0)
    y_ref[...] = dot(act_ref[:, 0:POOL_WIDTH], wpo_ref[...])
    g_ref[...] = dot(h_ref[...], win_ref[:, O_GPOOL:O_GPOOL + D_MODEL])
    merge(True)

    z_ref[:, 0:1536] = dot(h_ref[...], win_ref[:, O_CVAL:O_CVAL + 1536])
    rg = 32

    def glu_body(i, c):
        r = _rows(i, rg)
        cbuf[pl.ds(pl.multiple_of(i * rg + HIST_C, SUBLANES), rg), :] = (
            z_ref[r, 0:CONV_WIDTH] * _sigmoid(z_ref[r, CONV_WIDTH:2 * CONV_WIDTH]))
        return c

    lax.fori_loop(0, tm // rg, glu_body, 0)
    rcv = 64
    for lg in range(CONV_WIDTH // 128):
        lanes = slice(lg * 128, (lg + 1) * 128)

        def conv_body(i, c, lanes=lanes):
            row0 = pl.multiple_of(i * rcv, rcv)
            acc = jnp.broadcast_to(cb_ref[:, lanes], (rcv, 128))
            for k in range(CONV_KERNEL):
                acc = acc + cbuf[pl.ds(pl.multiple_of(row0 + BATCH * k, SUBLANES), rcv), lanes] * cdw_ref[k:k + 1, lanes]
            y_ref[pl.ds(row0, rcv), lanes] = acc
            return c

        lax.fori_loop(0, tm // rcv, conv_body, 0)
    rl = 32

    def ln_body(i, c):
        r = _rows(i, rl)
        v = y_ref[r, 0:CONV_WIDTH]
        mu = jnp.mean(v, axis=-1, keepdims=True)
        d = v - mu
        var = jnp.mean(d * d, axis=-1, keepdims=True)
        n = d * lax.rsqrt(var + EPS) * lng_ref[...] + lnb_ref[...]
        a = _silu(n) * _silu(z_ref[r, 2 * CONV_WIDTH:3 * CONV_WIDTH])
        act_ref[r, 0:CONV_WIDTH] = a.astype(BF16)
        return c

    lax.fori_loop(0, tm // rl, ln_body, 0)
    y_ref[...] = dot(act_ref[:, 0:CONV_WIDTH], wco_ref[...])
    g_ref[...] = dot(h_ref[...], win_ref[:, O_GCONV:O_GCONV + D_MODEL])
    merge(False)

    z_ref[...] = dot(h_ref[...], win_ref[:, O_RVAL:O_RVAL + 2048])
    rbuf[HIST_R:HIST_R + tm, :] = z_ref[:, 0:LRU_WIDTH]
    rr = 32

    def rconv_body(i, c):
        row0 = pl.multiple_of(i * rr, rr)
        acc = jnp.broadcast_to(rcb_ref[...], (rr, LRU_WIDTH))
        for k in range(LRU_CONV):
            acc = acc + rbuf[pl.ds(pl.multiple_of(row0 + BATCH * k, SUBLANES), rr), :] * rcw_ref[k:k + 1, :]
        rc_ref[pl.ds(row0, rr), :] = acc
        act_ref[pl.ds(row0, rr), :] = acc.astype(BF16)
        return c

    lax.fori_loop(0, tm // rr, rconv_body, 0)
    for hd in range(LRU_HEADS):
        cols = slice(hd * LRU_BLOCK, (hd + 1) * LRU_BLOCK)
        y_ref[:, cols] = dot(act_ref[:, cols], wa_ref[hd])
        g_ref[:, cols] = dot(act_ref[:, cols], wx_ref[hd])

    lam = lam_ref[...]
    cvec = -LRU_C * (jnp.maximum(-lam, 0.0) + jnp.log(1.0 + jnp.exp(-jnp.abs(lam))))

    def gate_body(i, c):
        r = _rows(i, rr)
        rg_ = _sigmoid(y_ref[r, :] + ba_ref[...])
        ig_ = _sigmoid(g_ref[r, :] + bx_ref[...])
        log_a = rg_ * cvec
        a = jnp.exp(log_a)
        mult = jnp.sqrt(-jnp.tanh(log_a) * (1.0 + a * a))
        y_ref[r, :] = a
        g_ref[r, :] = mult * (ig_ * rc_ref[r, :])
        return c

    lax.fori_loop(0, tm // rr, gate_body, 0)

    def scan_body(t, hs):
        r = _rows(t, SUBLANES)
        hs = y_ref[r, :] * hs + g_ref[r, :]
        g_ref[r, :] = hs
        return hs

    hs_ref[...] = lax.fori_loop(0, TILE_T, scan_body, hs_ref[...], unroll=8)

    def lru_act_body(i, c):
        r = _rows(i, rr)
        act_ref[r, :] = (g_ref[r, :] * _silu(z_ref[r, LRU_WIDTH:2 * LRU_WIDTH])).astype(BF16)
        return c

    lax.fori_loop(0, tm // rr, lru_act_body, 0)
    y_ref[...] = dot(act_ref[...], wlo_ref[...])
    g_ref[...] = dot(h_ref[...], win_ref[:, O_GLRU:O_GLRU + D_MODEL])
    merge(False)

    def mcast_body(i, c):
        r = _rows(i, rr)
        act_ref[r, :] = m_ref[r, :].astype(BF16)
        return c

    lax.fori_loop(0, tm // rr, mcast_body, 0)
    y_ref[...] = dot(act_ref[...], wout_ref[...])

    def out_body(i, c):
        r = _rows(i, rr)
        v = y_ref[r, :]
        ms = jnp.mean(v * v, axis=-1, keepdims=True)
        o_ref[r, :] = x_ref[r, :] + v * lax.rsqrt(ms + EPS) * npost_ref[...]
        return c

    lax.fori_loop(0, tm // rr, out_body, 0)

    pbuf[0:HIST_P, :] = pbuf[tm:tm + HIST_P, :]
    cbuf[0:HIST_C, :] = cbuf[tm:tm + HIST_C, :]
    rbuf[0:HIST_R, :] = rbuf[tm:tm + HIST_R, :]


def _const_spec(shape, layer):
    nd = len(shape)
    return pl.BlockSpec((None,) + tuple(shape[1:]), lambda i: (layer,) + (0,) * (nd - 1),
                        pipeline_mode=pl.Buffered(1))


def _layer_call(layer, xt, params):
    rows = xt.shape[0]
    tm = TILE_M
    in_specs = [pl.BlockSpec((tm, D_MODEL), lambda i: (i, 0))]
    in_specs += [_const_spec(p.shape, layer) for p in params]
    scratch = [
        pltpu.VMEM((tm, D_MODEL), BF16),
        pltpu.VMEM((tm, 2048), F32),
        pltpu.VMEM((tm, D_MODEL), F32),
        pltpu.VMEM((tm, D_MODEL), F32),
        pltpu.VMEM((tm, D_MODEL), F32),
        pltpu.VMEM((tm, D_MODEL), BF16),
        pltpu.VMEM((tm, LRU_WIDTH), F32),
        pltpu.VMEM((HIST_P + tm, POOL_WIDTH), F32),
        pltpu.VMEM((HIST_C + tm, CONV_WIDTH), F32),
        pltpu.VMEM((HIST_R + tm, LRU_WIDTH), F32),
        pltpu.VMEM((SUBLANES, LRU_WIDTH), F32),
    ]
    return pl.pallas_call(
        _layer_kernel,
        grid=(rows // tm,),
        in_specs=in_specs,
        out_specs=pl.BlockSpec((tm, D_MODEL), lambda i: (i, 0)),
        out_shape=jax.ShapeDtypeStruct((rows, D_MODEL), F32),
        scratch_shapes=scratch,
        compiler_params=pltpu.CompilerParams(
            dimension_semantics=("arbitrary",),
            vmem_limit_bytes=VMEM_LIMIT_BYTES),
        name=f"hybrid_layer_{layer}",
    )(xt, *params)


def kernel(x, norm_pre, w_in, pool_w, pool_scale, w_pool_out, conv_dw, conv_b, conv_ln_g, conv_ln_b, w_conv_out, lru_conv_w, lru_conv_b, lru_wa, lru_ba, lru_wx, lru_bx, lru_lambda, w_lru_out, w_out, norm_post):
    bsz, seq, d = x.shape
    depth = w_in.shape[0]
    assert (bsz, d) == (BATCH, D_MODEL) and seq % TILE_T == 0

    def vec(p):
        return p.reshape(depth, 1, p.shape[-1])

    pw_bd = jnp.zeros((depth, POOL_WIDTH, POOL_WIDTH), BF16)
    for grp in range(POOL_WIDTH // POOL_GROUP):
        sl = slice(grp * POOL_GROUP, (grp + 1) * POOL_GROUP)
        pw_bd = pw_bd.at[:, sl, sl].set(pool_w[:, grp].astype(BF16))

    params = [
        vec(norm_pre), w_in.astype(BF16), pw_bd, vec(pool_scale), w_pool_out.astype(BF16),
        conv_dw, vec(conv_b), vec(conv_ln_g), vec(conv_ln_b), w_conv_out.astype(BF16),
        lru_conv_w, vec(lru_conv_b), lru_wa.astype(BF16), vec(lru_ba), lru_wx.astype(BF16),
        vec(lru_bx), vec(lru_lambda), w_lru_out.astype(BF16), w_out.astype(BF16), vec(norm_post),
    ]
    xt = x.transpose(1, 0, 2).reshape(seq * bsz, d)
    for layer in range(depth):
        xt = _layer_call(layer, xt, params)
    return xt.reshape(seq, bsz, d).transpose(1, 0, 2)
```
